```python
import jax, jax.numpy as jnp
from jax import lax
import numpy as np

D_MODEL = 1024
BATCH = 8
SEQ = 2048
DEPTH = 4

HEAD_DIM = 128
N_Q_HEADS = D_MODEL // HEAD_DIM
N_KV_HEADS = N_Q_HEADS // 4
W_ATT = N_Q_HEADS * HEAD_DIM
W_KV = N_KV_HEADS * HEAD_DIM
W_SC = D_MODEL
SC_WIDTH = 3
W_CF = D_MODEL
CF_WIDTH = 31
GRID_W = 64
Q_BLOCK = 128
ROPE_THETA = 10000.0
RMS_EPS = 1e-6
LN_EPS = 1e-5
SPLITS = (W_SC, W_SC, W_SC, W_SC,
          W_ATT, W_KV, W_KV, W_ATT,
          W_CF, W_CF, W_CF,
          D_MODEL, D_MODEL, D_MODEL)
P_IN = sum(SPLITS)

kernel_name = "hybrid_conv_gqa_conformer_encoder"


def rms_norm(x, g):
    x32 = x.astype(jnp.float32)
    y = x32 * lax.rsqrt(jnp.mean(x32 * x32, axis=-1, keepdims=True) + RMS_EPS)
    return (y * g.astype(jnp.float32)).astype(x.dtype)


def layer_norm(x, g, b):
    x32 = x.astype(jnp.float32)
    mu = jnp.mean(x32, axis=-1, keepdims=True)
    xc = x32 - mu
    y = xc * lax.rsqrt(jnp.mean(xc * xc, axis=-1, keepdims=True) + LN_EPS)
    return (y * g.astype(jnp.float32) + b.astype(jnp.float32)).astype(x.dtype)


def depthwise_conv(x, w):
    k = w.shape[0]
    return lax.conv_general_dilated(
        x, w[:, None, :], window_strides=(1,), padding=[(k // 2, k // 2)],
        dimension_numbers=("NWC", "WIO", "NWC"), feature_group_count=x.shape[-1])


def axial_rope_tables(seq_len):
    n_rows = seq_len // GRID_W
    rows = jnp.repeat(jnp.arange(n_rows), GRID_W)
    cols = jnp.tile(jnp.arange(GRID_W), n_rows)
    n_freq = HEAD_DIM // 4
    inv_freq = ROPE_THETA ** (-jnp.arange(n_freq, dtype=jnp.float32) / n_freq)
    pos = jnp.stack([rows, cols], axis=-1).astype(jnp.float32)
    ang = pos[:, :, None] * inv_freq
    return jnp.cos(ang), jnp.sin(ang)


def apply_axial_rope(x, cos, sin):
    b, s, h, d = x.shape
    n_freq = d // 4
    xr = x.astype(jnp.float32).reshape(b, s, h, 2, 2, n_freq)
    x1, x2 = xr[..., 0, :], xr[..., 1, :]
    c = cos[None, :, None]
    sn = sin[None, :, None]
    out = jnp.stack([x1 * c - x2 * sn, x2 * c + x1 * sn], axis=-2)
    return out.reshape(b, s, h, d).astype(x.dtype)


def block_gqa_attention(q, k, v):
    b, s, hq, hd = q.shape
    hkv = k.shape[2]
    g = hq // hkv
    nb = s // Q_BLOCK
    qb = q.reshape(b, nb, Q_BLOCK, hkv, g, hd).transpose(1, 0, 2, 3, 4, 5)
    scale = hd ** -0.5

    def one_block(qi):
        sc = jnp.einsum("bqkgd,bskd->bkgqs", qi, k).astype(jnp.float32) * scale
        p = jax.nn.softmax(sc, axis=-1).astype(v.dtype)
        return jnp.einsum("bkgqs,bskd->bqkgd", p, v)

    o = lax.map(one_block, qb)
    return o.transpose(1, 0, 2, 3, 4, 5).reshape(b, s, hq * hd)


def hybrid_layer(x, cos, sin, norm_pre, norm_post, w_in, conv_a_w, q_norm, k_norm,
                 conv_c_w, conv_c_b, ln_c_g, ln_c_b, w_out_a, w_out_b, w_out_c, w_o):
    b, s, _ = x.shape
    h = rms_norm(x, norm_pre)
    proj = h @ w_in
    offsets = []
    acc = 0
    for w in SPLITS[:-1]:
        acc += w
        offsets.append(acc)
    (a_b, a_c, a_x, a_g, q, k, v, b_g, c_u, c_v, c_g,
     m_a, m_b, m_c) = jnp.split(proj, offsets, axis=-1)

    ya = a_b * depthwise_conv(a_c * a_x, conv_a_w)
    ya = (ya * jax.nn.silu(a_g)) @ w_out_a

    q = apply_axial_rope(rms_norm(q.reshape(b, s, N_Q_HEADS, HEAD_DIM), q_norm), cos, sin)
    k = apply_axial_rope(rms_norm(k.reshape(b, s, N_KV_HEADS, HEAD_DIM), k_norm), cos, sin)
    v = v.reshape(b, s, N_KV_HEADS, HEAD_DIM)
    yb = (block_gqa_attention(q, k, v) * jax.nn.silu(b_g)) @ w_out_b

    u = c_u * jax.nn.sigmoid(c_v)
    u = depthwise_conv(u, conv_c_w) + conv_c_b
    u = jax.nn.silu(layer_norm(u, ln_c_g, ln_c_b))
    yc = (u * jax.nn.silu(c_g)) @ w_out_c

    y = jax.nn.sigmoid(m_a) * ya + jax.nn.sigmoid(m_b) * yb + jax.nn.sigmoid(m_c) * yc
    return x + rms_norm(y @ w_o, norm_post)


def setup_inputs(seed: int = 0) -> dict:
    key = jax.random.key(seed)
    ks = jax.random.split(key, 16)
    f32 = jnp.float32
    L, D = DEPTH, D_MODEL

    def nrm(k, shape, scale):
        return jax.random.normal(k, shape, f32) * scale

    return {
        "x": nrm(ks[0], (BATCH, SEQ, D), 1.0),
        "norm_pre": 1.0 + nrm(ks[1], (L, D), 0.05),
        "norm_post": 1.0 + nrm(ks[2], (L, D), 0.05),
        "w_in": nrm(ks[3], (L, D, P_IN), D ** -0.5),
        "conv_a_w": nrm(ks[4], (L, SC_WIDTH, W_SC), SC_WIDTH ** -0.5),
        "q_norm": 1.0 + nrm(ks[5], (L, HEAD_DIM), 0.05),
        "k_norm": 1.0 + nrm(ks[6], (L, HEAD_DIM), 0.05),
        "conv_c_w": nrm(ks[7], (L, CF_WIDTH, W_CF), CF_WIDTH ** -0.5),
        "conv_c_b": nrm(ks[8], (L, W_CF), 0.02),
        "ln_c_g": 1.0 + nrm(ks[9], (L, W_CF), 0.05),
        "ln_c_b": nrm(ks[10], (L, W_CF), 0.02),
        "w_out_a": nrm(ks[11], (L, W_SC, D), W_SC ** -0.5),
        "w_out_b": nrm(ks[12], (L, W_ATT, D), W_ATT ** -0.5),
        "w_out_c": nrm(ks[13], (L, W_CF, D), W_CF ** -0.5),
        "w_o": nrm(ks[14], (L, D, D), D ** -0.5),
    }


def reference(x, norm_pre, norm_post, w_in, conv_a_w, q_norm, k_norm, conv_c_w, conv_c_b,
              ln_c_g, ln_c_b, w_out_a, w_out_b, w_out_c, w_o):
    cos, sin = axial_rope_tables(x.shape[1])
    for l in range(DEPTH):
        x = hybrid_layer(x, cos, sin, norm_pre[l], norm_post[l], w_in[l], conv_a_w[l],
                         q_norm[l], k_norm[l], conv_c_w[l], conv_c_b[l], ln_c_g[l], ln_c_b[l],
                         w_out_a[l], w_out_b[l], w_out_c[l], w_o[l])
    return x
```

```python
import functools

import jax
import jax.numpy as jnp
from jax import lax
from jax.experimental import pallas as pl
from jax.experimental.pallas import tpu as pltpu

D_MODEL = 1024
HEAD_DIM = 128
N_Q_HEADS = 8
N_KV_HEADS = 2
GQA_GROUP = N_Q_HEADS // N_KV_HEADS
SC_WIDTH = 3
CF_WIDTH = 31
GRID_W = 64
ROPE_THETA = 10000.0
RMS_EPS = 1e-6
LN_EPS = 1e-5

OFF_AB, OFF_AC, OFF_AX, OFF_AG = 0, 1024, 2048, 3072
OFF_Q, OFF_K, OFF_BG = 4096, 5120, 5632
OFF_CU, OFF_CV, OFF_CG = 6656, 7680, 8704
OFF_MA, OFF_MB, OFF_MC = 9728, 10752, 11776

LANES = 128
SUBLANES = 8
VMEM_LIMIT_BYTES = 60 * 1024 * 1024

CHUNK = 256
N_CHUNKS = D_MODEL // CHUNK
ROW_BLOCK = 512
CONV_ROWS = 64
PAD = 16

ATT_TQ = 256
OUT_ROWS = 512

F32 = jnp.float32
BF16 = jnp.bfloat16


def _sigmoid(x):
    return 1.0 / (1.0 + jnp.exp(-x))


def _silu(x):
    return x * _sigmoid(x)


def _head_norm_rope(x, gain, cos, sin_signed):
    ms = jnp.mean(x * x, axis=-1, keepdims=True)
    xn = x * lax.rsqrt(ms + RMS_EPS) * gain
    lane = lax.broadcasted_iota(jnp.int32, xn.shape, 1)
    first_half = (lane % (HEAD_DIM // 2)) < (HEAD_DIM // 4)
    partner = jnp.where(first_half,
                        pltpu.roll(xn, HEAD_DIM - HEAD_DIM // 4, 1),
                        pltpu.roll(xn, HEAD_DIM // 4, 1))
    return xn * cos + partner * sin_signed


def _prenorm_kernel(x_ref, g_ref, h_ref):
    x = x_ref[...]
    ms = jnp.mean(x * x, axis=-1, keepdims=True)
    h_ref[...] = (x * lax.rsqrt(ms + RMS_EPS) * g_ref[...]).astype(BF16)


def _prenorm(x2d, gain):
    t = x2d.shape[0]
    rows = 1024
    return pl.pallas_call(
        _prenorm_kernel,
        grid=(t // rows,),
        in_specs=[pl.BlockSpec((rows, D_MODEL), lambda i: (i, 0)),
                  pl.BlockSpec((1, D_MODEL), lambda i: (0, 0))],
        out_specs=pl.BlockSpec((rows, D_MODEL), lambda i: (i, 0)),
        out_shape=jax.ShapeDtypeStruct((t, D_MODEL), BF16),
        name="prenorm",
    )(x2d, gain)


def _in_proj_kernel(h_ref, wab, wac, wax, wag, wq, wkv, wbg, wcu, wcv, wcg, wma, wmb, wmc,
                    caw_ref, ccw_ref, ccb_ref, qn_ref, kn_ref, cos_ref, sin_ref,
                    apre_ref, q_ref, kv_ref, gb_ref, uc_ref, gc_ref, sa_ref, sb_ref, sc_ref,
                    pbuf, ubuf, gabuf, *, seq):
    c = pl.program_id(0)
    scale = HEAD_DIM ** -0.5

    zeros = jnp.zeros((PAD, CHUNK), F32)
    pbuf[pl.ds(0, PAD), :] = zeros
    pbuf[pl.ds(PAD + seq, PAD), :] = zeros
    ubuf[pl.ds(0, PAD), :] = zeros
    ubuf[pl.ds(PAD + seq, PAD), :] = zeros

    def stage1(rb, carry):
        r0 = pl.multiple_of(rb * ROW_BLOCK, ROW_BLOCK)
        rows = pl.ds(r0, ROW_BLOCK)
        hs = h_ref[rows, :]

        def mm(w_ref):
            return jnp.dot(hs, w_ref[...], preferred_element_type=F32)

        cos = cos_ref[rows, :]
        sin = sin_ref[rows, :]

        pbuf[pl.ds(PAD + r0, ROW_BLOCK), :] = mm(wac) * mm(wax)
        gabuf[rows, :] = mm(wab) * _silu(mm(wag))

        qf = mm(wq)
        for hh in range(CHUNK // HEAD_DIM):
            lanes = slice(hh * HEAD_DIM, (hh + 1) * HEAD_DIM)
            qh = _head_norm_rope(qf[:, lanes], qn_ref[...], cos, sin)
            q_ref[rows, lanes] = (qh * scale).astype(BF16)

        kvf = mm(wkv)
        kproc = _head_norm_rope(kvf, kn_ref[...], cos, sin)
        kv_ref[rows, :] = jnp.where(c < N_KV_HEADS, kproc, kvf).astype(BF16)

        gb_ref[rows, :] = _silu(mm(wbg)).astype(BF16)
        ubuf[pl.ds(PAD + r0, ROW_BLOCK), :] = mm(wcu) * _sigmoid(mm(wcv))
        gc_ref[rows, :] = _silu(mm(wcg)).astype(BF16)
        sa_ref[rows, :] = _sigmoid(mm(wma)).astype(BF16)
        sb_ref[rows, :] = _sigmoid(mm(wmb)).astype(BF16)
        sc_ref[rows, :] = _sigmoid(mm(wmc)).astype(BF16)
        return carry

    lax.fori_loop(0, seq // ROW_BLOCK, stage1, 0)

    def stage2(cb, carry):
        r0 = pl.multiple_of(cb * CONV_ROWS, CONV_ROWS)
        rows = pl.ds(r0, CONV_ROWS)
        uwin = ubuf[pl.ds(r0, CONV_ROWS + 2 * PAD), :]
        acc = jnp.zeros((CONV_ROWS, CHUNK), F32) + ccb_ref[...]
        for k in range(CF_WIDTH):
            off = PAD + k - CF_WIDTH // 2
            acc = acc + uwin[off:off + CONV_ROWS, :] * ccw_ref[k:k + 1, :]
        uc_ref[rows, :] = acc.astype(BF16)

        pwin = pbuf[pl.ds(r0, CONV_ROWS + 2 * PAD), :]
        acc3 = jnp.zeros((CONV_ROWS, CHUNK), F32)
        for k in range(SC_WIDTH):
            off = PAD + k - SC_WIDTH // 2
            acc3 = acc3 + pwin[off:off + CONV_ROWS, :] * caw_ref[k:k + 1, :]
        apre_ref[rows, :] = (gabuf[rows, :] * acc3).astype(BF16)
        return carry

    lax.fori_loop(0, seq // CONV_ROWS, stage2, 0)


def _in_proj(h, w_in, conv_a_w, conv_c_w, conv_c_b, q_norm, k_norm, cos, sin, *, batch, seq):
    t = batch * seq

    def wspec(off):
        first = off // CHUNK
        return pl.BlockSpec((D_MODEL, CHUNK), lambda c, i: (0, first + c))

    kv_first = OFF_K // HEAD_DIM
    in_specs = [
        pl.BlockSpec((seq, D_MODEL), lambda c, i: (i, 0)),
        wspec(OFF_AB), wspec(OFF_AC), wspec(OFF_AX), wspec(OFF_AG), wspec(OFF_Q),
        pl.BlockSpec((D_MODEL, HEAD_DIM), lambda c, i: (0, kv_first + c)),
        wspec(OFF_BG), wspec(OFF_CU), wspec(OFF_CV), wspec(OFF_CG),
        wspec(OFF_MA), wspec(OFF_MB), wspec(OFF_MC),
        pl.BlockSpec((SC_WIDTH, CHUNK), lambda c, i: (0, c)),
        pl.BlockSpec((CF_WIDTH, CHUNK), lambda c, i: (0, c)),
        pl.BlockSpec((1, CHUNK), lambda c, i: (0, c)),
        pl.BlockSpec((1, HEAD_DIM), lambda c, i: (0, 0)),
        pl.BlockSpec((1, HEAD_DIM), lambda c, i: (0, 0)),
        pl.BlockSpec((seq, HEAD_DIM), lambda c, i: (0, 0)),
        pl.BlockSpec((seq, HEAD_DIM), lambda c, i: (0, 0)),
    ]
    wide = pl.BlockSpec((seq, CHUNK), lambda c, i: (i, c))
    out_specs = [wide, wide, pl.BlockSpec((seq, HEAD_DIM), lambda c, i: (i, c)),
                 wide, wide, wide, wide, wide, wide]
    wide_shape = jax.ShapeDtypeStruct((t, D_MODEL), BF16)
    out_shape = [wide_shape, wide_shape,
                 jax.ShapeDtypeStruct((t, 2 * N_KV_HEADS * HEAD_DIM), BF16),
                 wide_shape, wide_shape, wide_shape, wide_shape, wide_shape, wide_shape]
    return pl.pallas_call(
        functools.partial(_in_proj_kernel, seq=seq),
        grid=(N_CHUNKS, batch),
        in_specs=in_specs,
        out_specs=out_specs,
        out_shape=out_shape,
        scratch_shapes=[pltpu.VMEM((seq + 2 * PAD, CHUNK), F32),
                        pltpu.VMEM((seq + 2 * PAD, CHUNK), F32),
                        pltpu.VMEM((seq, CHUNK), F32)],
        compiler_params=pltpu.CompilerParams(
            dimension_semantics=("arbitrary", "arbitrary"),
            vmem_limit_bytes=VMEM_LIMIT_BYTES),
        name="in_proj",
    )(h, *([w_in] * 13), conv_a_w, conv_c_w, conv_c_b, q_norm, k_norm, cos, sin)


def _attention_kernel(q_ref, k_ref, v_ref, gb_ref, o_ref):
    for g in range(N_KV_HEADS):
        kv_lanes = slice(g * HEAD_DIM, (g + 1) * HEAD_DIM)
        kg = k_ref[:, kv_lanes]
        vg = v_ref[:, kv_lanes]
        for hh in range(GQA_GROUP):
            lanes = slice((g * GQA_GROUP + hh) * HEAD_DIM, (g * GQA_GROUP + hh + 1) * HEAD_DIM)
            s = lax.dot_general(q_ref[:, lanes], kg, (((1,), (1,)), ((), ())),
                                preferred_element_type=F32)
            m = jnp.max(s, axis=-1, keepdims=True)
            p = jnp.exp(s - m)
            denom = jnp.sum(p, axis=-1, keepdims=True)
            o = jnp.dot(p.astype(BF16), vg, preferred_element_type=F32) / denom
            o_ref[:, lanes] = (o * gb_ref[:, lanes].astype(F32)).astype(BF16)


def _attention(q, kv, gb, *, batch, seq):
    t = batch * seq
    nq = seq // ATT_TQ
    kv_width = N_KV_HEADS * HEAD_DIM
    return pl.pallas_call(
        _attention_kernel,
        grid=(batch, nq),
        in_specs=[pl.BlockSpec((ATT_TQ, D_MODEL), lambda b, i: (b * nq + i, 0)),
                  pl.BlockSpec((seq, kv_width), lambda b, i: (b, 0)),
                  pl.BlockSpec((seq, kv_width), lambda b, i: (b, 1)),
                  pl.BlockSpec((ATT_TQ, D_MODEL), lambda b, i: (b * nq + i, 0))],
        out_specs=pl.BlockSpec((ATT_TQ, D_MODEL), lambda b, i: (b * nq + i, 0)),
        out_shape=jax.ShapeDtypeStruct((t, D_MODEL), BF16),
        compiler_params=pltpu.CompilerParams(
            dimension_semantics=("arbitrary", "arbitrary"),
            vmem_limit_bytes=VMEM_LIMIT_BYTES),
        name="attention",
    )(q, kv, kv, gb)


def _out_proj_kernel(x_ref, apre_ref, bpre_ref, uc_ref, gc_ref, sa_ref, sb_ref, sc_ref,
                     lng_ref, lnb_ref, woa_ref, wob_ref, woc_ref, wo_ref, npost_ref, npre_ref,
                     xo_ref, *maybe_h_ref):
    ya = jnp.dot(apre_ref[...], woa_ref[...], preferred_element_type=F32)
    yb = jnp.dot(bpre_ref[...], wob_ref[...], preferred_element_type=F32)

    u = uc_ref[...].astype(F32)
    mu = jnp.mean(u, axis=-1, keepdims=True)
    uc = u - mu
    var = jnp.mean(uc * uc, axis=-1, keepdims=True)
    un = uc * lax.rsqrt(var + LN_EPS) * lng_ref[...] + lnb_ref[...]
    cpre = _silu(un) * gc_ref[...].astype(F32)
    yc = jnp.dot(cpre.astype(BF16), woc_ref[...], preferred_element_type=F32)

    y = (sa_ref[...].astype(F32) * ya + sb_ref[...].astype(F32) * yb
         + sc_ref[...].astype(F32) * yc)
    z = jnp.dot(y.astype(BF16), wo_ref[...], preferred_element_type=F32)
    ms = jnp.mean(z * z, axis=-1, keepdims=True)
    xo = x_ref[...] + z * lax.rsqrt(ms + RMS_EPS) * npost_ref[...]
    xo_ref[...] = xo
    if maybe_h_ref:
        ms2 = jnp.mean(xo * xo, axis=-1, keepdims=True)
        maybe_h_ref[0][...] = (xo * lax.rsqrt(ms2 + RMS_EPS) * npre_ref[...]).astype(BF16)


def _out_proj(x2d, apre, bpre, uc, gc, sa, sb, sc, ln_g, ln_b, woa, wob, woc, wo,
              norm_post, norm_pre_next, *, emit_h):
    t = x2d.shape[0]
    tile = pl.BlockSpec((OUT_ROWS, D_MODEL), lambda i: (i, 0))
    vec = pl.BlockSpec((1, D_MODEL), lambda i: (0, 0))
    mat = pl.BlockSpec((D_MODEL, D_MODEL), lambda i: (0, 0))
    out_specs = [tile]
    out_shape = [jax.ShapeDtypeStruct((t, D_MODEL), F32)]
    if emit_h:
        out_specs.append(tile)
        out_shape.append(jax.ShapeDtypeStruct((t, D_MODEL), BF16))
    return pl.pallas_call(
        _out_proj_kernel,
        grid=(t // OUT_ROWS,),
        in_specs=[tile] * 8 + [vec, vec, mat, mat, mat, mat, vec, vec],
        out_specs=out_specs,
        out_shape=out_shape,
        compiler_params=pltpu.CompilerParams(
            dimension_semantics=("arbitrary",),
            vmem_limit_bytes=VMEM_LIMIT_BYTES),
        name="out_proj",
    )(x2d, apre, bpre, uc, gc, sa, sb, sc, ln_g, ln_b, woa, wob, woc, wo,
      norm_post, norm_pre_next)


def _rope_tables(seq):
    n_freq = HEAD_DIM // 4
    t = jnp.arange(seq)
    pos = jnp.stack([t // GRID_W, t % GRID_W], axis=-1).astype(F32)
    inv_freq = ROPE_THETA ** (-jnp.arange(n_freq, dtype=F32) / n_freq)
    ang = pos[:, :, None] * inv_freq
    cos = jnp.cos(ang)
    sin = jnp.sin(ang)
    cos_full = jnp.stack([cos, cos], axis=2).reshape(seq, HEAD_DIM)
    sin_signed = jnp.stack([-sin, sin], axis=2).reshape(seq, HEAD_DIM)
    return cos_full, sin_signed


def kernel(x, norm_pre, norm_post, w_in, conv_a_w, q_norm, k_norm, conv_c_w, conv_c_b,
           ln_c_g, ln_c_b, w_out_a, w_out_b, w_out_c, w_o):
    batch, seq, d = x.shape
    depth = w_in.shape[0]
    assert d == D_MODEL and seq % ROW_BLOCK == 0 and seq % ATT_TQ == 0
    assert (batch * seq) % OUT_ROWS == 0
    cos, sin = _rope_tables(seq)
    x2d = x.reshape(batch * seq, d)
    h = _prenorm(x2d, norm_pre[0][None, :])
    for l in range(depth):
        apre, q, kv, gb, uc, gc, sa, sb, sc = _in_proj(
            h, w_in[l].astype(BF16), conv_a_w[l], conv_c_w[l], conv_c_b[l][None, :],
            q_norm[l][None, :], k_norm[l][None, :], cos, sin, batch=batch, seq=seq)
        bpre = _attention(q, kv, gb, batch=batch, seq=seq)
        last = l == depth - 1
        outs = _out_proj(
            x2d, apre, bpre, uc, gc, sa, sb, sc, ln_c_g[l][None, :], ln_c_b[l][None, :],
            w_out_a[l].astype(BF16), w_out_b[l].astype(BF16), w_out_c[l].astype(BF16),
            w_o[l].astype(BF16), norm_post[l][None, :],
            norm_pre[(l + 1) % depth][None, :], emit_h=not last)
        x2d = outs[0]
        if not last:
            h = outs[1]
    return x2d.reshape(batch, seq, d)
```

```python
import functools

import jax
import jax.numpy as jnp
from jax import lax
from jax.experimental import pallas as pl
from jax.experimental.pallas import tpu as pltpu

D_MODEL = 1024
HEAD_DIM = 128
N_Q_HEADS = 8
N_KV_HEADS = 2
GQA_GROUP = N_Q_HEADS // N_KV_HEADS
SC_WIDTH = 3
CF_WIDTH = 31
GRID_W = 64
ROPE_THETA = 10000.0
RMS_EPS = 1e-6
LN_EPS = 1e-5

OFF_AB, OFF_AC, OFF_AX, OFF_AG = 0, 1024, 2048, 3072
OFF_Q, OFF_K, OFF_BG = 4096, 5120, 5632
OFF_CU, OFF_CV, OFF_CG = 6656, 7680, 8704
OFF_MA, OFF_MB, OFF_MC = 9728, 10752, 11776

LANES = 128
SUBLANES = 8
VMEM_LIMIT_BYTES = 60 * 1024 * 1024

CHUNK = 256
N_CHUNKS = D_MODEL // CHUNK
N_IN_WEIGHTS = 13
ROW_BLOCK = 512
CONV_ROWS = 64
PAD = 16

ATT_TQ = 256
OUT_ROWS = 512

F32 = jnp.float32
BF16 = jnp.bfloat16

QK_SCALE = HEAD_DIM ** -0.5 * 1.4426950408889634


def _sigmoid(x):
    return 1.0 / (1.0 + jnp.exp(-x))


def _silu(x):
    return x * _sigmoid(x)


def _head_norm_rope(x, gain, cos, sin_signed):
    ms = jnp.mean(x * x, axis=-1, keepdims=True)
    xn = x * lax.rsqrt(ms + RMS_EPS) * gain
    lane = lax.broadcasted_iota(jnp.int32, xn.shape, 1)
    first_half = (lane % (HEAD_DIM // 2)) < (HEAD_DIM // 4)
    partner = jnp.where(first_half,
                        pltpu.roll(xn, HEAD_DIM - HEAD_DIM // 4, 1),
                        pltpu.roll(xn, HEAD_DIM // 4, 1))
    return xn * cos + partner * sin_signed


def _prenorm_kernel(x_ref, g_ref, h_ref):
    x = x_ref[...]
    ms = jnp.mean(x * x, axis=-1, keepdims=True)
    h_ref[...] = (x * lax.rsqrt(ms + RMS_EPS) * g_ref[...]).astype(BF16)


def _prenorm(x2d, gain):
    t = x2d.shape[0]
    rows = 1024
    return pl.pallas_call(
        _prenorm_kernel,
        grid=(t // rows,),
        in_specs=[pl.BlockSpec((rows, D_MODEL), lambda i: (i, 0)),
                  pl.BlockSpec((1, D_MODEL), lambda i: (0, 0))],
        out_specs=pl.BlockSpec((rows, D_MODEL), lambda i: (i, 0)),
        out_shape=jax.ShapeDtypeStruct((t, D_MODEL), BF16),
        name="prenorm",
    )(x2d, gain)


def _conv_block(c0, pbuf, ubuf, gabuf, caw_ref, ccw_ref, ccb_ref, apre_ref, uc_ref):
    rows = pl.ds(c0, CONV_ROWS)
    for lh in range(CHUNK // LANES):
        lanes = slice(lh * LANES, (lh + 1) * LANES)
        win_rows = CONV_ROWS + 2 * PAD
        uwin = ubuf[pl.ds(c0, win_rows), lanes]
        acc = jnp.zeros((CONV_ROWS, LANES), F32) + ccb_ref[:, lanes]
        for b in range(SUBLANES):
            shifted = uwin if b == 0 else pltpu.roll(uwin, win_rows - b, 0)
            for a in range((2 * PAD) // SUBLANES):
                k = a * SUBLANES + b - (PAD - CF_WIDTH // 2)
                if 0 <= k < CF_WIDTH:
                    acc = acc + (shifted[a * SUBLANES:a * SUBLANES + CONV_ROWS, :]
                                 * ccw_ref[k:k + 1, lanes])
        uc_ref[rows, lanes] = acc.astype(BF16)

        pwin = pbuf[pl.ds(c0 + PAD - SUBLANES, CONV_ROWS + 2 * SUBLANES), lanes]
        acc3 = jnp.zeros((CONV_ROWS, LANES), F32)
        for k in range(SC_WIDTH):
            off = SUBLANES + k - SC_WIDTH // 2
            acc3 = acc3 + pwin[off:off + CONV_ROWS, :] * caw_ref[k:k + 1, lanes]
        apre_ref[rows, lanes] = (gabuf[rows, lanes] * acc3).astype(BF16)


def _in_proj_kernel(h_ref, wab, wac, wax, wag, wq, wkv, wbg, wcu, wcv, wcg, wma, wmb, wmc,
                    caw_ref, ccw_ref, ccb_ref, qn_ref, kn_ref, cos_ref, sin_ref,
                    apre_ref, q_ref, kv_ref, gb_ref, uc_ref, gc_ref, sa_ref, sb_ref, sc_ref,
                    pbuf, ubuf, gabuf, *, seq):
    c = pl.program_id(0)

    zeros = jnp.zeros((PAD, CHUNK), F32)
    pbuf[pl.ds(0, PAD), :] = zeros
    pbuf[pl.ds(PAD + seq, PAD), :] = zeros
    ubuf[pl.ds(0, PAD), :] = zeros
    ubuf[pl.ds(PAD + seq, PAD), :] = zeros

    conv = functools.partial(_conv_block, pbuf=pbuf, ubuf=ubuf, gabuf=gabuf, caw_ref=caw_ref,
                             ccw_ref=ccw_ref, ccb_ref=ccb_ref, apre_ref=apre_ref, uc_ref=uc_ref)

    def row_block(rb, carry):
        r0 = pl.multiple_of(rb * ROW_BLOCK, ROW_BLOCK)
        rows = pl.ds(r0, ROW_BLOCK)
        hs = h_ref[rows, :]

        def mm(w_ref):
            return jnp.dot(hs, w_ref[...], preferred_element_type=F32)

        cos = cos_ref[rows, :]
        sin = sin_ref[rows, :]

        pbuf[pl.ds(PAD + r0, ROW_BLOCK), :] = mm(wac) * mm(wax)
        gabuf[rows, :] = mm(wab) * _silu(mm(wag))
        ubuf[pl.ds(PAD + r0, ROW_BLOCK), :] = mm(wcu) * _sigmoid(mm(wcv))

        qf = mm(wq)
        for hh in range(CHUNK // HEAD_DIM):
            lanes = slice(hh * HEAD_DIM, (hh + 1) * HEAD_DIM)
            qh = _head_norm_rope(qf[:, lanes], qn_ref[...], cos, sin)
            q_ref[rows, lanes] = (qh * QK_SCALE).astype(BF16)

        kvf = mm(wkv)
        kproc = _head_norm_rope(kvf, kn_ref[...], cos, sin)
        kv_ref[rows, :] = jnp.where(c < N_KV_HEADS, kproc, kvf).astype(BF16)

        gb_ref[rows, :] = _silu(mm(wbg)).astype(BF16)
        gc_ref[rows, :] = _silu(mm(wcg)).astype(BF16)
        sa_ref[rows, :] = _sigmoid(mm(wma)).astype(BF16)
        sb_ref[rows, :] = _sigmoid(mm(wmb)).astype(BF16)
        sc_ref[rows, :] = _sigmoid(mm(wmc)).astype(BF16)

        for j in range(ROW_BLOCK // CONV_ROWS):
            c0 = jnp.maximum(r0 + (j - 1) * CONV_ROWS, 0) if j == 0 else r0 + (j - 1) * CONV_ROWS
            conv(pl.multiple_of(c0, CONV_ROWS))
        return carry

    lax.fori_loop(0, seq // ROW_BLOCK, row_block, 0)
    conv(seq - CONV_ROWS)


def _in_proj(h, w_in, conv_a_w, conv_c_w, conv_c_b, q_norm, k_norm, cos, sin, *, batch, seq):
    t = batch * seq

    def wspec(off):
        first = off // CHUNK
        return pl.BlockSpec((D_MODEL, CHUNK), lambda c, i: (0, first + c))

    kv_first = OFF_K // HEAD_DIM
    in_specs = [
        pl.BlockSpec((seq, D_MODEL), lambda c, i: (i, 0)),
        wspec(OFF_AB), wspec(OFF_AC), wspec(OFF_AX), wspec(OFF_AG), wspec(OFF_Q),
        pl.BlockSpec((D_MODEL, HEAD_DIM), lambda c, i: (0, kv_first + c)),
        wspec(OFF_BG), wspec(OFF_CU), wspec(OFF_CV), wspec(OFF_CG),
        wspec(OFF_MA), wspec(OFF_MB), wspec(OFF_MC),
        pl.BlockSpec((SC_WIDTH, CHUNK), lambda c, i: (0, c)),
        pl.BlockSpec((CF_WIDTH, CHUNK), lambda c, i: (0, c)),
        pl.BlockSpec((1, CHUNK), lambda c, i: (0, c)),
        pl.BlockSpec((1, HEAD_DIM), lambda c, i: (0, 0)),
        pl.BlockSpec((1, HEAD_DIM), lambda c, i: (0, 0)),
        pl.BlockSpec((seq, HEAD_DIM), lambda c, i: (0, 0)),
        pl.BlockSpec((seq, HEAD_DIM), lambda c, i: (0, 0)),
    ]
    wide = pl.BlockSpec((seq, CHUNK), lambda c, i: (i, c))
    out_specs = [wide, wide, pl.BlockSpec((seq, HEAD_DIM), lambda c, i: (i, c)),
                 wide, wide, wide, wide, wide, wide]
    wide_shape = jax.ShapeDtypeStruct((t, D_MODEL), BF16)
    out_shape = [wide_shape, wide_shape,
                 jax.ShapeDtypeStruct((t, 2 * N_KV_HEADS * HEAD_DIM), BF16),
                 wide_shape, wide_shape, wide_shape, wide_shape, wide_shape, wide_shape]
    return pl.pallas_call(
        functools.partial(_in_proj_kernel, seq=seq),
        grid=(N_CHUNKS, batch),
        in_specs=in_specs,
        out_specs=out_specs,
        out_shape=out_shape,
        scratch_shapes=[pltpu.VMEM((seq + 2 * PAD, CHUNK), F32),
                        pltpu.VMEM((seq + 2 * PAD, CHUNK), F32),
                        pltpu.VMEM((seq, CHUNK), F32)],
        compiler_params=pltpu.CompilerParams(
            dimension_semantics=("arbitrary", "arbitrary"),
            vmem_limit_bytes=VMEM_LIMIT_BYTES),
        name="in_proj",
    )(h, *([w_in] * N_IN_WEIGHTS), conv_a_w, conv_c_w, conv_c_b, q_norm, k_norm, cos, sin)


def _attention_kernel(q_ref, k_ref, v_ref, gb_ref, o_ref, vext_ref):
    @pl.when(pl.program_id(1) == 0)
    def _():
        for g in range(N_KV_HEADS):
            vext_ref[g, :, :HEAD_DIM] = v_ref[:, g * HEAD_DIM:(g + 1) * HEAD_DIM]
            vext_ref[g, :, HEAD_DIM:] = jnp.ones((v_ref.shape[0], HEAD_DIM), BF16)

    for g in range(N_KV_HEADS):
        kg = k_ref[:, g * HEAD_DIM:(g + 1) * HEAD_DIM]
        vg = vext_ref[g]
        for hh in range(GQA_GROUP):
            lanes = slice((g * GQA_GROUP + hh) * HEAD_DIM, (g * GQA_GROUP + hh + 1) * HEAD_DIM)
            s = lax.dot_general(q_ref[:, lanes], kg, (((1,), (1,)), ((), ())),
                                preferred_element_type=F32)
            m = jnp.max(s, axis=-1, keepdims=True)
            p = jnp.exp2(s - m).astype(BF16)
            acc = jnp.dot(p, vg, preferred_element_type=F32)
            o = acc[:, :HEAD_DIM] / acc[:, HEAD_DIM:]
            o_ref[:, lanes] = (o * gb_ref[:, lanes].astype(F32)).astype(BF16)


def _attention(q, kv, gb, *, batch, seq):
    t = batch * seq
    nq = seq // ATT_TQ
    kv_width = N_KV_HEADS * HEAD_DIM
    return pl.pallas_call(
        _attention_kernel,
        grid=(batch, nq),
        in_specs=[pl.BlockSpec((ATT_TQ, D_MODEL), lambda b, i: (b * nq + i, 0)),
                  pl.BlockSpec((seq, kv_width), lambda b, i: (b, 0)),
                  pl.BlockSpec((seq, kv_width), lambda b, i: (b, 1)),
                  pl.BlockSpec((ATT_TQ, D_MODEL), lambda b, i: (b * nq + i, 0))],
        out_specs=pl.BlockSpec((ATT_TQ, D_MODEL), lambda b, i: (b * nq + i, 0)),
        out_shape=jax.ShapeDtypeStruct((t, D_MODEL), BF16),
        scratch_shapes=[pltpu.VMEM((N_KV_HEADS, seq, 2 * HEAD_DIM), BF16)],
        compiler_params=pltpu.CompilerParams(
            dimension_semantics=("arbitrary", "arbitrary"),
            vmem_limit_bytes=VMEM_LIMIT_BYTES),
        name="attention",
    )(q, kv, kv, gb)


def _out_proj_kernel(x_ref, apre_ref, bpre_ref, uc_ref, gc_ref, sa_ref, sb_ref, sc_ref,
                     lng_ref, lnb_ref, woa_ref, wob_ref, woc_ref, wo_ref, npost_ref, npre_ref,
                     xo_ref, *maybe_h_ref):
    ya = jnp.dot(apre_ref[...], woa_ref[...], preferred_element_type=F32)
    yb = jnp.dot(bpre_ref[...], wob_ref[...], preferred_element_type=F32)

    u = uc_ref[...].astype(F32)
    mu = jnp.mean(u, axis=-1, keepdims=True)
    uc = u - mu
    var = jnp.mean(uc * uc, axis=-1, keepdims=True)
    un = uc * lax.rsqrt(var + LN_EPS) * lng_ref[...] + lnb_ref[...]
    cpre = _silu(un) * gc_ref[...].astype(F32)
    yc = jnp.dot(cpre.astype(BF16), woc_ref[...], preferred_element_type=F32)

    y = (sa_ref[...].astype(F32) * ya + sb_ref[...].astype(F32) * yb
         + sc_ref[...].astype(F32) * yc)
    z = jnp.dot(y.astype(BF16), wo_ref[...], preferred_element_type=F32)
    ms = jnp.mean(z * z, axis=-1, keepdims=True)
    xo = x_ref[...] + z * lax.rsqrt(ms + RMS_EPS) * npost_ref[...]
    xo_ref[...] = xo
    if maybe_h_ref:
        ms2 = jnp.mean(xo * xo, axis=-1, keepdims=True)
        maybe_h_ref[0][...] = (xo * lax.rsqrt(ms2 + RMS_EPS) * npre_ref[...]).astype(BF16)


def _out_proj(x2d, apre, bpre, uc, gc, sa, sb, sc, ln_g, ln_b, woa, wob, woc, wo,
              norm_post, norm_pre_next, *, emit_h):
    t = x2d.shape[0]
    tile = pl.BlockSpec((OUT_ROWS, D_MODEL), lambda i: (i, 0))
    vec = pl.BlockSpec((1, D_MODEL), lambda i: (0, 0))
    mat = pl.BlockSpec((D_MODEL, D_MODEL), lambda i: (0, 0))
    out_specs = [tile]
    out_shape = [jax.ShapeDtypeStruct((t, D_MODEL), F32)]
    if emit_h:
        out_specs.append(tile)
        out_shape.append(jax.ShapeDtypeStruct((t, D_MODEL), BF16))
    return pl.pallas_call(
        _out_proj_kernel,
        grid=(t // OUT_ROWS,),
        in_specs=[tile] * 8 + [vec, vec, mat, mat, mat, mat, vec, vec],
        out_specs=out_specs,
        out_shape=out_shape,
        compiler_params=pltpu.CompilerParams(
            dimension_semantics=("arbitrary",),
            vmem_limit_bytes=VMEM_LIMIT_BYTES),
        name="out_proj",
    )(x2d, apre, bpre, uc, gc, sa, sb, sc, ln_g, ln_b, woa, wob, woc, wo,
      norm_post, norm_pre_next)


def _rope_tables(seq):
    n_freq = HEAD_DIM // 4
    t = jnp.arange(seq)
    pos = jnp.stack([t // GRID_W, t % GRID_W], axis=-1).astype(F32)
    inv_freq = ROPE_THETA ** (-jnp.arange(n_freq, dtype=F32) / n_freq)
    ang = pos[:, :, None] * inv_freq
    cos = jnp.cos(ang)
    sin = jnp.sin(ang)
    cos_full = jnp.stack([cos, cos], axis=2).reshape(seq, HEAD_DIM)
    sin_signed = jnp.stack([-sin, sin], axis=2).reshape(seq, HEAD_DIM)
    return cos_full, sin_signed


def kernel(x, norm_pre, norm_post, w_in, conv_a_w, q_norm, k_norm, conv_c_w, conv_c_b,
           ln_c_g, ln_c_b, w_out_a, w_out_b, w_out_c, w_o):
    batch, seq, d = x.shape
    depth = w_in.shape[0]
    assert d == D_MODEL and seq % ROW_BLOCK == 0 and seq % ATT_TQ == 0
    assert (batch * seq) % OUT_ROWS == 0
    cos, sin = _rope_tables(seq)
    x2d = x.reshape(batch * seq, d)
    h = _prenorm(x2d, norm_pre[0][None, :])
    for l in range(depth):
        apre, q, kv, gb, uc, gc, sa, sb, sc = _in_proj(
            h, w_in[l].astype(BF16), conv_a_w[l], conv_c_w[l], conv_c_b[l][None, :],
            q_norm[l][None, :], k_norm[l][None, :], cos, sin, batch=batch, seq=seq)
        bpre = _attention(q, kv, gb, batch=batch, seq=seq)
        last = l == depth - 1
        outs = _out_proj(
            x2d, apre, bpre, uc, gc, sa, sb, sc, ln_c_g[l][None, :], ln_c_b[l][None, :],
            w_out_a[l].astype(BF16), w_out_b[l].astype(BF16), w_out_c[l].astype(BF16),
            w_o[l].astype(BF16), norm_post[l][None, :],
            norm_pre[(l + 1) % depth][None, :], emit_h=not last)
        x2d = outs[0]
        if not last:
            h = outs[1]
    return x2d.reshape(batch, seq, d)
```

```python
import functools

import jax
import jax.numpy as jnp
from jax import lax
from jax.experimental import pallas as pl
from jax.experimental.pallas import tpu as pltpu

D_MODEL = 1024
HEAD_DIM = 128
N_Q_HEADS = 8
N_KV_HEADS = 2
GQA_GROUP = N_Q_HEADS // N_KV_HEADS
SC_WIDTH = 3
CF_WIDTH = 31
GRID_W = 64
ROPE_THETA = 10000.0
RMS_EPS = 1e-6
LN_EPS = 1e-5

OFF_AB, OFF_AC, OFF_AX, OFF_AG = 0, 1024, 2048, 3072
OFF_Q, OFF_K, OFF_BG = 4096, 5120, 5632
OFF_CU, OFF_CV, OFF_CG = 6656, 7680, 8704
OFF_MA, OFF_MB, OFF_MC = 9728, 10752, 11776

LANES = 128
SUBLANES = 8
VMEM_LIMIT_BYTES = 60 * 1024 * 1024

CHUNK = 256
N_CHUNKS = D_MODEL // CHUNK
SEG_ORDER = (OFF_AC, OFF_AX, OFF_AB, OFF_AG, OFF_CU, OFF_CV,
             OFF_Q, OFF_BG, OFF_CG, OFF_MA, OFF_MB, OFF_MC)
W_KV = len(SEG_ORDER) * CHUNK
W_CHUNK_COLS = W_KV + HEAD_DIM
ROW_BLOCK = 512
CONV_ROWS = 64
PAD = 16

ATT_TQ = 512
OUT_ROWS = 512
OUT_SUB_ROWS = 256

F32 = jnp.float32
BF16 = jnp.bfloat16

QK_SCALE = HEAD_DIM ** -0.5 * 1.4426950408889634


def _sigmoid(x):
    return 1.0 / (1.0 + jnp.exp(-x))


def _silu(x):
    return x * _sigmoid(x)


def _head_norm_rope(x, gain, cos, sin_signed):
    ms = jnp.mean(x * x, axis=-1, keepdims=True)
    xn = x * lax.rsqrt(ms + RMS_EPS) * gain
    lane = lax.broadcasted_iota(jnp.int32, xn.shape, 1)
    first_half = (lane % (HEAD_DIM // 2)) < (HEAD_DIM // 4)
    partner = jnp.where(first_half,
                        pltpu.roll(xn, HEAD_DIM - HEAD_DIM // 4, 1),
                        pltpu.roll(xn, HEAD_DIM // 4, 1))
    return xn * cos + partner * sin_signed


def _prenorm_kernel(x_ref, g_ref, h_ref):
    x = x_ref[...]
    ms = jnp.mean(x * x, axis=-1, keepdims=True)
    h_ref[...] = (x * lax.rsqrt(ms + RMS_EPS) * g_ref[...]).astype(BF16)


def _prenorm(x2d, gain):
    t = x2d.shape[0]
    rows = 1024
    return pl.pallas_call(
        _prenorm_kernel,
        grid=(t // rows,),
        in_specs=[pl.BlockSpec((rows, D_MODEL), lambda i: (i, 0)),
                  pl.BlockSpec((1, D_MODEL), lambda i: (0, 0))],
        out_specs=pl.BlockSpec((rows, D_MODEL), lambda i: (i, 0)),
        out_shape=jax.ShapeDtypeStruct((t, D_MODEL), BF16),
        name="prenorm",
    )(x2d, gain)


def _conv_half(c0, lh, pbuf, ubuf, gabuf, caw_ref, ccw_ref, ccb_ref, apre_ref, uc_ref):
    rows = pl.ds(c0, CONV_ROWS)
    lanes = slice(lh * LANES, (lh + 1) * LANES)
    win_rows = CONV_ROWS + 2 * PAD
    uwin = ubuf[pl.ds(c0, win_rows), lanes]
    acc = jnp.zeros((CONV_ROWS, LANES), F32) + ccb_ref[:, lanes]
    for b in range(SUBLANES):
        shifted = uwin if b == 0 else pltpu.roll(uwin, win_rows - b, 0)
        for a in range((2 * PAD) // SUBLANES):
            k = a * SUBLANES + b - (PAD - CF_WIDTH // 2)
            if 0 <= k < CF_WIDTH:
                acc = acc + (shifted[a * SUBLANES:a * SUBLANES + CONV_ROWS, :]
                             * ccw_ref[k:k + 1, lanes])
    uc_ref[rows, lanes] = acc.astype(BF16)

    pwin = pbuf[pl.ds(c0 + PAD - SUBLANES, CONV_ROWS + 2 * SUBLANES), lanes]
    acc3 = jnp.zeros((CONV_ROWS, LANES), F32)
    for k in range(SC_WIDTH):
        off = SUBLANES + k - SC_WIDTH // 2
        acc3 = acc3 + pwin[off:off + CONV_ROWS, :] * caw_ref[k:k + 1, lanes]
    apre_ref[rows, lanes] = (gabuf[rows, lanes] * acc3).astype(BF16)


def _in_proj_kernel(h_ref, w_ref, caw_ref, ccw_ref, ccb_ref, qn_ref, kn_ref, cos_ref, sin_ref,
                    apre_ref, q_ref, kv_ref, gb_ref, uc_ref, gc_ref, sa_ref, sb_ref, sc_ref,
                    pbuf, ubuf, gabuf, *, seq):
    c = pl.program_id(0)

    zeros = jnp.zeros((PAD, CHUNK), F32)
    pbuf[pl.ds(0, PAD), :] = zeros
    pbuf[pl.ds(PAD + seq, PAD), :] = zeros
    ubuf[pl.ds(0, PAD), :] = zeros
    ubuf[pl.ds(PAD + seq, PAD), :] = zeros

    conv_half = functools.partial(_conv_half, pbuf=pbuf, ubuf=ubuf, gabuf=gabuf, caw_ref=caw_ref,
                                  ccw_ref=ccw_ref, ccb_ref=ccb_ref, apre_ref=apre_ref,
                                  uc_ref=uc_ref)

    def conv(c0):
        for lh in range(CHUNK // LANES):
            conv_half(c0, lh)

    def row_block(rb, carry):
        r0 = pl.multiple_of(rb * ROW_BLOCK, ROW_BLOCK)
        rows = pl.ds(r0, ROW_BLOCK)
        hs = h_ref[rows, :]

        def mm(n):
            return jnp.dot(hs, w_ref[:, n * CHUNK:(n + 1) * CHUNK], preferred_element_type=F32)

        cos = cos_ref[rows, :]
        sin = sin_ref[rows, :]

        pbuf[pl.ds(PAD + r0, ROW_BLOCK), :] = mm(0) * mm(1)
        gabuf[rows, :] = mm(2) * _silu(mm(3))
        ubuf[pl.ds(PAD + r0, ROW_BLOCK), :] = mm(4) * _sigmoid(mm(5))

        qf = mm(6)
        for hh in range(CHUNK // HEAD_DIM):
            lanes = slice(hh * HEAD_DIM, (hh + 1) * HEAD_DIM)
            qh = _head_norm_rope(qf[:, lanes], qn_ref[...], cos, sin)
            q_ref[rows, lanes] = (qh * QK_SCALE).astype(BF16)

        kvf = jnp.dot(hs, w_ref[:, W_KV:], preferred_element_type=F32)
        kproc = _head_norm_rope(kvf, kn_ref[...], cos, sin)
        kv_ref[rows, :] = jnp.where(c < N_KV_HEADS, kproc, kvf).astype(BF16)

        gb_ref[rows, :] = _silu(mm(7)).astype(BF16)
        gc_ref[rows, :] = _silu(mm(8)).astype(BF16)
        sa_ref[rows, :] = _sigmoid(mm(9)).astype(BF16)
        sb_ref[rows, :] = _sigmoid(mm(10)).astype(BF16)
        sc_ref[rows, :] = _sigmoid(mm(11)).astype(BF16)

        for j in range(ROW_BLOCK // CONV_ROWS):
            c0 = jnp.maximum(r0 + (j - 1) * CONV_ROWS, 0) if j == 0 else r0 + (j - 1) * CONV_ROWS
            conv(pl.multiple_of(c0, CONV_ROWS))
        return carry

    lax.fori_loop(0, seq // ROW_BLOCK, row_block, 0)
    conv(seq - CONV_ROWS)


def _in_proj(layer, h, w_perm, conv_a_w, conv_c_w, conv_c_b, q_norm, k_norm, cos, sin,
             *, batch, seq):
    t = batch * seq
    in_specs = [
        pl.BlockSpec((seq, D_MODEL), lambda c, i: (i, 0)),
        pl.BlockSpec((None, D_MODEL, W_CHUNK_COLS), lambda c, i: (layer, 0, c)),
        pl.BlockSpec((SC_WIDTH, CHUNK), lambda c, i: (0, c)),
        pl.BlockSpec((CF_WIDTH, CHUNK), lambda c, i: (0, c)),
        pl.BlockSpec((1, CHUNK), lambda c, i: (0, c)),
        pl.BlockSpec((1, HEAD_DIM), lambda c, i: (0, 0)),
        pl.BlockSpec((1, HEAD_DIM), lambda c, i: (0, 0)),
        pl.BlockSpec((seq, HEAD_DIM), lambda c, i: (0, 0)),
        pl.BlockSpec((seq, HEAD_DIM), lambda c, i: (0, 0)),
    ]
    wide = pl.BlockSpec((seq, CHUNK), lambda c, i: (i, c))
    out_specs = [wide, wide, pl.BlockSpec((seq, HEAD_DIM), lambda c, i: (i, c)),
                 wide, wide, wide, wide, wide, wide]
    wide_shape = jax.ShapeDtypeStruct((t, D_MODEL), BF16)
    out_shape = [wide_shape, wide_shape,
                 jax.ShapeDtypeStruct((t, 2 * N_KV_HEADS * HEAD_DIM), BF16),
                 wide_shape, wide_shape, wide_shape, wide_shape, wide_shape, wide_shape]
    return pl.pallas_call(
        functools.partial(_in_proj_kernel, seq=seq),
        grid=(N_CHUNKS, batch),
        in_specs=in_specs,
        out_specs=out_specs,
        out_shape=out_shape,
        scratch_shapes=[pltpu.VMEM((seq + 2 * PAD, CHUNK), F32),
                        pltpu.VMEM((seq + 2 * PAD, CHUNK), F32),
                        pltpu.VMEM((seq, CHUNK), F32)],
        compiler_params=pltpu.CompilerParams(
            dimension_semantics=("arbitrary", "arbitrary"),
            vmem_limit_bytes=VMEM_LIMIT_BYTES),
        name="in_proj",
    )(h, w_perm, conv_a_w, conv_c_w, conv_c_b, q_norm, k_norm, cos, sin)


def _attention_kernel(q_ref, k_ref, v_ref, gb_ref, o_ref, vext_ref):
    @pl.when(pl.program_id(1) == 0)
    def _():
        for g in range(N_KV_HEADS):
            vext_ref[g, :, :HEAD_DIM] = v_ref[:, g * HEAD_DIM:(g + 1) * HEAD_DIM]
            vext_ref[g, :, HEAD_DIM:] = jnp.ones((v_ref.shape[0], HEAD_DIM), BF16)

    for g in range(N_KV_HEADS):
        kg = k_ref[:, g * HEAD_DIM:(g + 1) * HEAD_DIM]
        vg = vext_ref[g]
        for hh in range(GQA_GROUP):
            lanes = slice((g * GQA_GROUP + hh) * HEAD_DIM, (g * GQA_GROUP + hh + 1) * HEAD_DIM)
            s = lax.dot_general(q_ref[:, lanes], kg, (((1,), (1,)), ((), ())),
                                preferred_element_type=F32)
            m = jnp.max(s, axis=-1, keepdims=True)
            p = jnp.exp2(s - m).astype(BF16)
            acc = jnp.dot(p, vg, preferred_element_type=F32)
            o = acc[:, :HEAD_DIM] / acc[:, HEAD_DIM:]
            o_ref[:, lanes] = (o * gb_ref[:, lanes].astype(F32)).astype(BF16)


def _attention(q, kv, gb, *, batch, seq):
    t = batch * seq
    nq = seq // ATT_TQ
    kv_width = N_KV_HEADS * HEAD_DIM
    return pl.pallas_call(
        _attention_kernel,
        grid=(batch, nq),
        in_specs=[pl.BlockSpec((ATT_TQ, D_MODEL), lambda b, i: (b * nq + i, 0)),
                  pl.BlockSpec((seq, kv_width), lambda b, i: (b, 0)),
                  pl.BlockSpec((seq, kv_width), lambda b, i: (b, 1)),
                  pl.BlockSpec((ATT_TQ, D_MODEL), lambda b, i: (b * nq + i, 0))],
        out_specs=pl.BlockSpec((ATT_TQ, D_MODEL), lambda b, i: (b * nq + i, 0)),
        out_shape=jax.ShapeDtypeStruct((t, D_MODEL), BF16),
        scratch_shapes=[pltpu.VMEM((N_KV_HEADS, seq, 2 * HEAD_DIM), BF16)],
        compiler_params=pltpu.CompilerParams(
            dimension_semantics=("arbitrary", "arbitrary"),
            vmem_limit_bytes=VMEM_LIMIT_BYTES),
        name="attention",
    )(q, kv, kv, gb)


def _out_proj_kernel(x_ref, apre_ref, bpre_ref, uc_ref, gc_ref, sa_ref, sb_ref, sc_ref,
                     lng_ref, lnb_ref, woa_ref, wob_ref, woc_ref, wo_ref, npost_ref, *rest):
    if len(rest) == 1:
        npre_ref, (xo_ref,), h_ref = None, rest, None
    else:
        npre_ref, xo_ref, h_ref = rest
    for st in range(OUT_ROWS // OUT_SUB_ROWS):
        rows = slice(st * OUT_SUB_ROWS, (st + 1) * OUT_SUB_ROWS)
        ya = jnp.dot(apre_ref[rows, :], woa_ref[...], preferred_element_type=F32)
        yb = jnp.dot(bpre_ref[rows, :], wob_ref[...], preferred_element_type=F32)

        u = uc_ref[rows, :].astype(F32)
        mu = jnp.mean(u, axis=-1, keepdims=True)
        uc = u - mu
        var = jnp.mean(uc * uc, axis=-1, keepdims=True)
        un = uc * lax.rsqrt(var + LN_EPS) * lng_ref[...] + lnb_ref[...]
        cpre = _silu(un) * gc_ref[rows, :].astype(F32)
        yc = jnp.dot(cpre.astype(BF16), woc_ref[...], preferred_element_type=F32)

        y = (sa_ref[rows, :].astype(F32) * ya + sb_ref[rows, :].astype(F32) * yb
             + sc_ref[rows, :].astype(F32) * yc)
        z = jnp.dot(y.astype(BF16), wo_ref[...], preferred_element_type=F32)
        ms = jnp.mean(z * z, axis=-1, keepdims=True)
        xo = x_ref[rows, :] + z * lax.rsqrt(ms + RMS_EPS) * npost_ref[...]
        xo_ref[rows, :] = xo
        if h_ref is not None:
            ms2 = jnp.mean(xo * xo, axis=-1, keepdims=True)
            h_ref[rows, :] = (xo * lax.rsqrt(ms2 + RMS_EPS) * npre_ref[...]).astype(BF16)


def _out_proj(x2d, apre, bpre, uc, gc, sa, sb, sc, ln_g, ln_b, woa, wob, woc, wo,
              norm_post, norm_pre_next):
    t = x2d.shape[0]
    emit_h = norm_pre_next is not None
    tile = pl.BlockSpec((OUT_ROWS, D_MODEL), lambda i: (i, 0))
    vec = pl.BlockSpec((1, D_MODEL), lambda i: (0, 0))
    mat = pl.BlockSpec((D_MODEL, D_MODEL), lambda i: (0, 0))
    in_specs = [tile] * 8 + [vec, vec, mat, mat, mat, mat, vec]
    operands = [x2d, apre, bpre, uc, gc, sa, sb, sc, ln_g, ln_b, woa, wob, woc, wo, norm_post]
    out_specs = [tile]
    out_shape = [jax.ShapeDtypeStruct((t, D_MODEL), F32)]
    if emit_h:
        in_specs.append(vec)
        operands.append(norm_pre_next)
        out_specs.append(tile)
        out_shape.append(jax.ShapeDtypeStruct((t, D_MODEL), BF16))
    return pl.pallas_call(
        _out_proj_kernel,
        grid=(t // OUT_ROWS,),
        in_specs=in_specs,
        out_specs=out_specs,
        out_shape=out_shape,
        compiler_params=pltpu.CompilerParams(
            dimension_semantics=("arbitrary",),
            vmem_limit_bytes=VMEM_LIMIT_BYTES),
        name="out_proj",
    )(*operands)


def _rope_tables(seq):
    n_freq = HEAD_DIM // 4
    t = jnp.arange(seq)
    pos = jnp.stack([t // GRID_W, t % GRID_W], axis=-1).astype(F32)
    inv_freq = ROPE_THETA ** (-jnp.arange(n_freq, dtype=F32) / n_freq)
    ang = pos[:, :, None] * inv_freq
    cos = jnp.cos(ang)
    sin = jnp.sin(ang)
    cos_full = jnp.stack([cos, cos], axis=2).reshape(seq, HEAD_DIM)
    sin_signed = jnp.stack([-sin, sin], axis=2).reshape(seq, HEAD_DIM)
    return cos_full, sin_signed


def _permute_w_in(w_in):
    depth = w_in.shape[0]
    parts = [w_in[:, :, off:off + D_MODEL].reshape(depth, D_MODEL, N_CHUNKS, CHUNK)
             for off in SEG_ORDER]
    parts.append(w_in[:, :, OFF_K:OFF_BG].reshape(depth, D_MODEL, N_CHUNKS, HEAD_DIM))
    return jnp.concatenate(parts, axis=-1).reshape(depth, D_MODEL, -1).astype(BF16)


def kernel(x, norm_pre, norm_post, w_in, conv_a_w, q_norm, k_norm, conv_c_w, conv_c_b,
           ln_c_g, ln_c_b, w_out_a, w_out_b, w_out_c, w_o):
    batch, seq, d = x.shape
    depth = w_in.shape[0]
    assert d == D_MODEL and seq % ROW_BLOCK == 0 and seq % ATT_TQ == 0
    assert (batch * seq) % OUT_ROWS == 0
    cos, sin = _rope_tables(seq)
    w_perm = _permute_w_in(w_in)
    x2d = x.reshape(batch * seq, d)
    h = _prenorm(x2d, norm_pre[0][None, :])
    for l in range(depth):
        apre, q, kv, gb, uc, gc, sa, sb, sc = _in_proj(
            l, h, w_perm, conv_a_w[l], conv_c_w[l], conv_c_b[l][None, :],
            q_norm[l][None, :], k_norm[l][None, :], cos, sin, batch=batch, seq=seq)
        bpre = _attention(q, kv, gb, batch=batch, seq=seq)
        last = l == depth - 1
        outs = _out_proj(
            x2d, apre, bpre, uc, gc, sa, sb, sc, ln_c_g[l][None, :], ln_c_b[l][None, :],
            w_out_a[l].astype(BF16), w_out_b[l].astype(BF16), w_out_c[l].astype(BF16),
            w_o[l].astype(BF16), norm_post[l][None, :],
            None if last else norm_pre[l + 1][None, :])
        x2d = outs[0]
        if not last:
            h = outs[1]
    return x2d.reshape(batch, seq, d)
```

```python
import functools

import jax
import jax.numpy as jnp
from jax import lax
from jax.experimental import pallas as pl
from jax.experimental.pallas import tpu as pltpu

D_MODEL = 1024
HEAD_DIM = 128
N_Q_HEADS = 8
N_KV_HEADS = 2
GQA_GROUP = N_Q_HEADS // N_KV_HEADS
SC_WIDTH = 3
CF_WIDTH = 31
GRID_W = 64
ROPE_THETA = 10000.0
RMS_EPS = 1e-6
LN_EPS = 1e-5

OFF_AB, OFF_AC, OFF_AX, OFF_AG = 0, 1024, 2048, 3072
OFF_Q, OFF_K, OFF_BG = 4096, 5120, 5632
OFF_CU, OFF_CV, OFF_CG = 6656, 7680, 8704
OFF_MA, OFF_MB, OFF_MC = 9728, 10752, 11776

LANES = 128
SUBLANES = 8
VMEM_LIMIT_BYTES = 60 * 1024 * 1024

CHUNK = 256
N_CHUNKS = D_MODEL // CHUNK
SEG_ORDER = (OFF_AC, OFF_AX, OFF_AB, OFF_AG, OFF_CU, OFF_CV,
             OFF_Q, OFF_BG, OFF_CG, OFF_MA, OFF_MB, OFF_MC)
ROW_BLOCK = 512
CONV_ROWS = 64
PAD = 16

ATT_TQ = 512
OUT_ROWS = 512
OUT_SUB_ROWS = 256

F32 = jnp.float32
BF16 = jnp.bfloat16

QK_SCALE = HEAD_DIM ** -0.5 * 1.4426950408889634


def _sigmoid(x):
    return 1.0 / (1.0 + jnp.exp(-x))


def _silu(x):
    return x * _sigmoid(x)


def _head_norm_rope(x, gain, cos, sin_signed):
    ms = jnp.mean(x * x, axis=-1, keepdims=True)
    xn = x * lax.rsqrt(ms + RMS_EPS) * gain
    lane = lax.broadcasted_iota(jnp.int32, xn.shape, 1)
    first_half = (lane % (HEAD_DIM // 2)) < (HEAD_DIM // 4)
    partner = jnp.where(first_half,
                        pltpu.roll(xn, HEAD_DIM - HEAD_DIM // 4, 1),
                        pltpu.roll(xn, HEAD_DIM // 4, 1))
    return xn * cos + partner * sin_signed


def _prenorm_kernel(x_ref, g_ref, h_ref):
    x = x_ref[...]
    ms = jnp.mean(x * x, axis=-1, keepdims=True)
    h_ref[...] = (x * lax.rsqrt(ms + RMS_EPS) * g_ref[...]).astype(BF16)


def _prenorm(x2d, gain):
    t = x2d.shape[0]
    rows = 1024
    return pl.pallas_call(
        _prenorm_kernel,
        grid=(t // rows,),
        in_specs=[pl.BlockSpec((rows, D_MODEL), lambda i: (i, 0)),
                  pl.BlockSpec((1, D_MODEL), lambda i: (0, 0))],
        out_specs=pl.BlockSpec((rows, D_MODEL), lambda i: (i, 0)),
        out_shape=jax.ShapeDtypeStruct((t, D_MODEL), BF16),
        name="prenorm",
    )(x2d, gain)


def _conv_half(c0, lh, pbuf, ubuf, gabuf, caw_ref, ccw_ref, ccb_ref, apre_ref, uc_ref):
    rows = pl.ds(c0, CONV_ROWS)
    lanes = slice(lh * LANES, (lh + 1) * LANES)
    win_rows = CONV_ROWS + 2 * PAD
    uwin = ubuf[pl.ds(c0, win_rows), lanes]
    acc = jnp.zeros((CONV_ROWS, LANES), F32) + ccb_ref[:, lanes]
    for b in range(SUBLANES):
        shifted = uwin if b == 0 else pltpu.roll(uwin, win_rows - b, 0)
        for a in range((2 * PAD) // SUBLANES):
            k = a * SUBLANES + b - (PAD - CF_WIDTH // 2)
            if 0 <= k < CF_WIDTH:
                acc = acc + (shifted[a * SUBLANES:a * SUBLANES + CONV_ROWS, :]
                             * ccw_ref[k:k + 1, lanes])
    uc_ref[rows, lanes] = acc.astype(BF16)

    pwin = pbuf[pl.ds(c0 + PAD - SUBLANES, CONV_ROWS + 2 * SUBLANES), lanes]
    acc3 = jnp.zeros((CONV_ROWS, LANES), F32)
    for k in range(SC_WIDTH):
        off = SUBLANES + k - SC_WIDTH // 2
        acc3 = acc3 + pwin[off:off + CONV_ROWS, :] * caw_ref[k:k + 1, lanes]
    apre_ref[rows, lanes] = (gabuf[rows, lanes] * acc3).astype(BF16)


def _in_proj_kernel(h_ref, *refs, seq):
    w_refs = refs[:len(SEG_ORDER)]
    (wkv_ref, caw_ref, ccw_ref, ccb_ref, qn_ref, kn_ref, cos_ref, sin_ref,
     apre_ref, q_ref, kv_ref, gb_ref, uc_ref, gc_ref, sa_ref, sb_ref, sc_ref,
     pbuf, ubuf, gabuf) = refs[len(SEG_ORDER):]
    c = pl.program_id(0)

    zeros = jnp.zeros((PAD, CHUNK), F32)
    pbuf[pl.ds(0, PAD), :] = zeros
    pbuf[pl.ds(PAD + seq, PAD), :] = zeros
    ubuf[pl.ds(0, PAD), :] = zeros
    ubuf[pl.ds(PAD + seq, PAD), :] = zeros

    conv_half = functools.partial(_conv_half, pbuf=pbuf, ubuf=ubuf, gabuf=gabuf, caw_ref=caw_ref,
                                  ccw_ref=ccw_ref, ccb_ref=ccb_ref, apre_ref=apre_ref,
                                  uc_ref=uc_ref)

    def conv(c0):
        for lh in range(CHUNK // LANES):
            conv_half(c0, lh)

    def row_block(rb, carry):
        r0 = pl.multiple_of(rb * ROW_BLOCK, ROW_BLOCK)
        rows = pl.ds(r0, ROW_BLOCK)
        hs = h_ref[rows, :]

        def mm(n):
            return jnp.dot(hs, w_refs[n][...], preferred_element_type=F32)

        cos = cos_ref[rows, :]
        sin = sin_ref[rows, :]

        pbuf[pl.ds(PAD + r0, ROW_BLOCK), :] = mm(0) * mm(1)
        gabuf[rows, :] = mm(2) * _silu(mm(3))
        ubuf[pl.ds(PAD + r0, ROW_BLOCK), :] = mm(4) * _sigmoid(mm(5))

        qf = mm(6)
        for hh in range(CHUNK // HEAD_DIM):
            lanes = slice(hh * HEAD_DIM, (hh + 1) * HEAD_DIM)
            qh = _head_norm_rope(qf[:, lanes], qn_ref[...], cos, sin)
            q_ref[rows, lanes] = (qh * QK_SCALE).astype(BF16)

        kvf = jnp.dot(hs, wkv_ref[...], preferred_element_type=F32)
        kproc = _head_norm_rope(kvf, kn_ref[...], cos, sin)
        kv_ref[rows, :] = jnp.where(c < N_KV_HEADS, kproc, kvf).astype(BF16)

        gb_ref[rows, :] = _silu(mm(7)).astype(BF16)
        gc_ref[rows, :] = _silu(mm(8)).astype(BF16)
        sa_ref[rows, :] = _sigmoid(mm(9)).astype(BF16)
        sb_ref[rows, :] = _sigmoid(mm(10)).astype(BF16)
        sc_ref[rows, :] = _sigmoid(mm(11)).astype(BF16)

        for j in range(ROW_BLOCK // CONV_ROWS):
            c0 = jnp.maximum(r0 + (j - 1) * CONV_ROWS, 0) if j == 0 else r0 + (j - 1) * CONV_ROWS
            conv(pl.multiple_of(c0, CONV_ROWS))
        return carry

    lax.fori_loop(0, seq // ROW_BLOCK, row_block, 0)
    conv(seq - CONV_ROWS)


def _in_proj(layer, h, w_in, conv_a_w, conv_c_w, conv_c_b, q_norm, k_norm, cos, sin,
             *, batch, seq):
    t = batch * seq

    def wspec(off, width):
        first = off // width
        return pl.BlockSpec((None, D_MODEL, width), lambda c, i: (layer, 0, first + c))

    in_specs = [
        pl.BlockSpec((seq, D_MODEL), lambda c, i: (i, 0)),
        *[wspec(off, CHUNK) for off in SEG_ORDER],
        wspec(OFF_K, HEAD_DIM),
        pl.BlockSpec((SC_WIDTH, CHUNK), lambda c, i: (0, c)),
        pl.BlockSpec((CF_WIDTH, CHUNK), lambda c, i: (0, c)),
        pl.BlockSpec((1, CHUNK), lambda c, i: (0, c)),
        pl.BlockSpec((1, HEAD_DIM), lambda c, i: (0, 0)),
        pl.BlockSpec((1, HEAD_DIM), lambda c, i: (0, 0)),
        pl.BlockSpec((seq, HEAD_DIM), lambda c, i: (0, 0)),
        pl.BlockSpec((seq, HEAD_DIM), lambda c, i: (0, 0)),
    ]
    wide = pl.BlockSpec((seq, CHUNK), lambda c, i: (i, c))
    out_specs = [wide, wide, pl.BlockSpec((seq, HEAD_DIM), lambda c, i: (i, c)),
                 wide, wide, wide, wide, wide, wide]
    wide_shape = jax.ShapeDtypeStruct((t, D_MODEL), BF16)
    out_shape = [wide_shape, wide_shape,
                 jax.ShapeDtypeStruct((t, 2 * N_KV_HEADS * HEAD_DIM), BF16),
                 wide_shape, wide_shape, wide_shape, wide_shape, wide_shape, wide_shape]
    return pl.pallas_call(
        functools.partial(_in_proj_kernel, seq=seq),
        grid=(N_CHUNKS, batch),
        in_specs=in_specs,
        out_specs=out_specs,
        out_shape=out_shape,
        scratch_shapes=[pltpu.VMEM((seq + 2 * PAD, CHUNK), F32),
                        pltpu.VMEM((seq + 2 * PAD, CHUNK), F32),
                        pltpu.VMEM((seq, CHUNK), F32)],
        compiler_params=pltpu.CompilerParams(
            dimension_semantics=("arbitrary", "arbitrary"),
            vmem_limit_bytes=VMEM_LIMIT_BYTES),
        name="in_proj",
    )(h, *([w_in] * (len(SEG_ORDER) + 1)), conv_a_w, conv_c_w, conv_c_b, q_norm, k_norm, cos, sin)


def _attention_kernel(q_ref, k_ref, v_ref, gb_ref, o_ref, vext_ref):
    @pl.when(pl.program_id(1) == 0)
    def _():
        for g in range(N_KV_HEADS):
            vext_ref[g, :, :HEAD_DIM] = v_ref[:, g * HEAD_DIM:(g + 1) * HEAD_DIM]
            vext_ref[g, :, HEAD_DIM:] = jnp.ones((v_ref.shape[0], HEAD_DIM), BF16)

    for g in range(N_KV_HEADS):
        kg = k_ref[:, g * HEAD_DIM:(g + 1) * HEAD_DIM]
        vg = vext_ref[g]
        for hh in range(GQA_GROUP):
            lanes = slice((g * GQA_GROUP + hh) * HEAD_DIM, (g * GQA_GROUP + hh + 1) * HEAD_DIM)
            s = lax.dot_general(q_ref[:, lanes], kg, (((1,), (1,)), ((), ())),
                                preferred_element_type=F32)
            m = jnp.max(s, axis=-1, keepdims=True)
            p = jnp.exp2(s - m).astype(BF16)
            acc = jnp.dot(p, vg, preferred_element_type=F32)
            o = acc[:, :HEAD_DIM] / acc[:, HEAD_DIM:]
            o_ref[:, lanes] = (o * gb_ref[:, lanes].astype(F32)).astype(BF16)


def _attention(q, kv, gb, *, batch, seq):
    t = batch * seq
    nq = seq // ATT_TQ
    kv_width = N_KV_HEADS * HEAD_DIM
    return pl.pallas_call(
        _attention_kernel,
        grid=(batch, nq),
        in_specs=[pl.BlockSpec((ATT_TQ, D_MODEL), lambda b, i: (b * nq + i, 0)),
                  pl.BlockSpec((seq, kv_width), lambda b, i: (b, 0)),
                  pl.BlockSpec((seq, kv_width), lambda b, i: (b, 1)),
                  pl.BlockSpec((ATT_TQ, D_MODEL), lambda b, i: (b * nq + i, 0))],
        out_specs=pl.BlockSpec((ATT_TQ, D_MODEL), lambda b, i: (b * nq + i, 0)),
        out_shape=jax.ShapeDtypeStruct((t, D_MODEL), BF16),
        scratch_shapes=[pltpu.VMEM((N_KV_HEADS, seq, 2 * HEAD_DIM), BF16)],
        compiler_params=pltpu.CompilerParams(
            dimension_semantics=("arbitrary", "arbitrary"),
            vmem_limit_bytes=VMEM_LIMIT_BYTES),
        name="attention",
    )(q, kv, kv, gb)


def _out_proj_kernel(x_ref, apre_ref, bpre_ref, uc_ref, gc_ref, sa_ref, sb_ref, sc_ref,
                     lng_ref, lnb_ref, woa_ref, wob_ref, woc_ref, wo_ref, npost_ref, *rest):
    if len(rest) == 1:
        npre_ref, (xo_ref,), h_ref = None, rest, None
    else:
        npre_ref, xo_ref, h_ref = rest
    for st in range(OUT_ROWS // OUT_SUB_ROWS):
        rows = slice(st * OUT_SUB_ROWS, (st + 1) * OUT_SUB_ROWS)
        ya = jnp.dot(apre_ref[rows, :], woa_ref[...], preferred_element_type=F32)
        yb = jnp.dot(bpre_ref[rows, :], wob_ref[...], preferred_element_type=F32)

        u = uc_ref[rows, :].astype(F32)
        mu = jnp.mean(u, axis=-1, keepdims=True)
        uc = u - mu
        var = jnp.mean(uc * uc, axis=-1, keepdims=True)
        un = uc * lax.rsqrt(var + LN_EPS) * lng_ref[...] + lnb_ref[...]
        cpre = _silu(un) * gc_ref[rows, :].astype(F32)
        yc = jnp.dot(cpre.astype(BF16), woc_ref[...], preferred_element_type=F32)

        y = (sa_ref[rows, :].astype(F32) * ya + sb_ref[rows, :].astype(F32) * yb
             + sc_ref[rows, :].astype(F32) * yc)
        z = jnp.dot(y.astype(BF16), wo_ref[...], preferred_element_type=F32)
        ms = jnp.mean(z * z, axis=-1, keepdims=True)
        xo = x_ref[rows, :] + z * lax.rsqrt(ms + RMS_EPS) * npost_ref[...]
        xo_ref[rows, :] = xo
        if h_ref is not None:
            ms2 = jnp.mean(xo * xo, axis=-1, keepdims=True)
            h_ref[rows, :] = (xo * lax.rsqrt(ms2 + RMS_EPS) * npre_ref[...]).astype(BF16)


def _out_proj(layer, x2d, apre, bpre, uc, gc, sa, sb, sc, ln_g, ln_b, woa, wob, woc, wo,
              norm_post, norm_pre_next):
    t = x2d.shape[0]
    emit_h = norm_pre_next is not None
    tile = pl.BlockSpec((OUT_ROWS, D_MODEL), lambda i: (i, 0))
    vec = pl.BlockSpec((1, D_MODEL), lambda i: (0, 0))
    mat = pl.BlockSpec((None, D_MODEL, D_MODEL), lambda i: (layer, 0, 0))
    in_specs = [tile] * 8 + [vec, vec, mat, mat, mat, mat, vec]
    operands = [x2d, apre, bpre, uc, gc, sa, sb, sc, ln_g, ln_b, woa, wob, woc, wo, norm_post]
    out_specs = [tile]
    out_shape = [jax.ShapeDtypeStruct((t, D_MODEL), F32)]
    if emit_h:
        in_specs.append(vec)
        operands.append(norm_pre_next)
        out_specs.append(tile)
        out_shape.append(jax.ShapeDtypeStruct((t, D_MODEL), BF16))
    return pl.pallas_call(
        _out_proj_kernel,
        grid=(t // OUT_ROWS,),
        in_specs=in_specs,
        out_specs=out_specs,
        out_shape=out_shape,
        compiler_params=pltpu.CompilerParams(
            dimension_semantics=("arbitrary",),
            vmem_limit_bytes=VMEM_LIMIT_BYTES),
        name="out_proj",
    )(*operands)


def _rope_tables(seq):
    n_freq = HEAD_DIM // 4
    t = jnp.arange(seq)
    pos = jnp.stack([t // GRID_W, t % GRID_W], axis=-1).astype(F32)
    inv_freq = ROPE_THETA ** (-jnp.arange(n_freq, dtype=F32) / n_freq)
    ang = pos[:, :, None] * inv_freq
    cos = jnp.cos(ang)
    sin = jnp.sin(ang)
    cos_full = jnp.stack([cos, cos], axis=2).reshape(seq, HEAD_DIM)
    sin_signed = jnp.stack([-sin, sin], axis=2).reshape(seq, HEAD_DIM)
    return cos_full, sin_signed


def kernel(x, norm_pre, norm_post, w_in, conv_a_w, q_norm, k_norm, conv_c_w, conv_c_b,
           ln_c_g, ln_c_b, w_out_a, w_out_b, w_out_c, w_o):
    batch, seq, d = x.shape
    depth = w_in.shape[0]
    assert d == D_MODEL and seq % ROW_BLOCK == 0 and seq % ATT_TQ == 0
    assert (batch * seq) % OUT_ROWS == 0
    cos, sin = _rope_tables(seq)
    w_in = w_in.astype(BF16)
    w_out = [w.astype(BF16) for w in (w_out_a, w_out_b, w_out_c, w_o)]
    x2d = x.reshape(batch * seq, d)
    h = _prenorm(x2d, norm_pre[0][None, :])
    for l in range(depth):
        apre, q, kv, gb, uc, gc, sa, sb, sc = _in_proj(
            l, h, w_in, conv_a_w[l], conv_c_w[l], conv_c_b[l][None, :],
            q_norm[l][None, :], k_norm[l][None, :], cos, sin, batch=batch, seq=seq)
        bpre = _attention(q, kv, gb, batch=batch, seq=seq)
        last = l == depth - 1
        outs = _out_proj(
            l, x2d, apre, bpre, uc, gc, sa, sb, sc, ln_c_g[l][None, :], ln_c_b[l][None, :],
            *w_out, norm_post[l][None, :],
            None if last else norm_pre[l + 1][None, :])
        x2d = outs[0]
        if not last:
            h = outs[1]
    return x2d.reshape(batch, seq, d)
```
